```python
import math
import jax
import jax.numpy as jnp
from jax import lax
import numpy as np

D_MODEL = 4096
BATCH = 4
SEQ = 2048
DEPTH = 2
DEC_BATCH = 8
DEC_SEQ = 8
PAST_LEN = 16384
PAGE_SIZE = 128

N_EVEN = (DEPTH + 1) // 2
N_ODD = DEPTH // 2
HEAD_DIM = 128
MIX_A = D_MODEL // 2
MIX_B = D_MODEL // 2
HGRN_HEADS = MIX_A // HEAD_DIM
HGRN_CHUNK = 64
MOBA_HEADS = MIX_B // HEAD_DIM
MOBA_BLOCK = 256
MOBA_TOPK = 3
MOBA_QBLOCK = 16
IN_EVEN = 4 * MIX_A + 3 * MIX_B
RWKV_HEAD = 64
RWKV_HEADS = D_MODEL // RWKV_HEAD
RWKV_DECAY_LORA = max(32, round(1.8 * D_MODEL ** 0.5 / 32) * 32)
RWKV_AAA_LORA = max(32, round(1.8 * D_MODEL ** 0.5 / 32) * 32)
RWKV_GATE_LORA = max(32, round(0.6 * D_MODEL ** 0.8 / 32) * 32)
RWKV_GN_EPS = 64e-5
MEM_TOKENS = 256
MEM_HEADS = 4
MEM_WIDTH = MEM_HEADS * HEAD_DIM
MOE_GROUPS = 8
MOE_PER_GROUP = 8
MOE_EXPERTS = MOE_GROUPS * MOE_PER_GROUP
MOE_TOPK = 2
MOE_HIDDEN = D_MODEL // 4
MOE_BLOCK_MAX = 128
MOE_BLOCK_MIN = 8
NORM_EPS = 1e-6

kernel_name = 'hgrn2_moba_rwkv7_hiermoe_decode_step'


def rmsnorm(x, g):
    xf = x.astype(jnp.float32)
    y = xf * lax.rsqrt(jnp.mean(xf * xf, axis=-1, keepdims=True) + NORM_EPS)
    return (y * g.astype(jnp.float32)).astype(x.dtype)


def split_heads(t, h):
    return t.reshape(t.shape[:-1] + (h, t.shape[-1] // h))


def alibi_slopes(n):
    return jnp.asarray(np.array([2.0 ** (-8.0 * (i + 1) / n) for i in range(n)], dtype=np.float32))


def hgrn2_recurrence(q, logf, k, v, s0):
    bsz, length, heads, _ = q.shape
    dv = v.shape[-1]
    c = math.gcd(HGRN_CHUNK, length)
    n = length // c

    def to_blocks(t):
        return t.astype(jnp.float32).reshape(bsz, n, c, heads, t.shape[-1]).transpose(1, 0, 3, 2, 4)

    causal = jnp.tril(jnp.ones((c, c), dtype=bool))

    def step(state, inp):
        qc, gc, kc, vc = inp
        bcum = jnp.cumsum(gc, axis=2)
        o = jnp.einsum('bhtk,bhkv->bhtv', qc * jnp.exp(bcum), state)
        diff = jnp.where(causal[None, None, :, :, None],
                         bcum[:, :, :, None, :] - bcum[:, :, None, :, :], -jnp.inf)
        att = jnp.einsum('bhtk,bhtsk,bhsk->bhts', qc, jnp.exp(diff), kc)
        o = o + jnp.einsum('bhts,bhsv->bhtv', att, vc)
        blast = bcum[:, :, -1]
        state = jnp.exp(blast)[..., None] * state + jnp.einsum(
            'bhsk,bhsv->bhkv', kc * jnp.exp(blast[:, :, None, :] - bcum), vc)
        return state, o

    s_fin, o = lax.scan(step, s0.astype(jnp.float32),
                        (to_blocks(q), to_blocks(logf), to_blocks(k), to_blocks(v)))
    o = o.transpose(1, 0, 3, 2, 4).reshape(bsz, length, heads, dv)
    return o, s_fin


def hgrn2_branch(qa, fa, ia, ga, lb, s0, g_out):
    q = split_heads(jax.nn.silu(qa.astype(jnp.float32)) * HEAD_DIM ** -0.5, HGRN_HEADS)
    f = lb + (1.0 - lb) * jax.nn.sigmoid(fa.astype(jnp.float32))
    logf = split_heads(jnp.log(f), HGRN_HEADS)
    k = split_heads(1.0 - f, HGRN_HEADS)
    v = split_heads(ia.astype(jnp.float32), HGRN_HEADS)
    o, s_new = hgrn2_recurrence(q, logf, k, v, s0)
    o = rmsnorm(o, g_out) * jax.nn.silu(split_heads(ga.astype(jnp.float32), HGRN_HEADS))
    return o.reshape(o.shape[:2] + (MIX_A,)).astype(qa.dtype), s_new


def moba_attention(q, k, v, q_pos, slopes):
    bsz, lq, heads, dh = q.shape
    lk = k.shape[1]
    nblk = -(-lk // MOBA_BLOCK)
    pad = ((0, 0), (0, nblk * MOBA_BLOCK - lk), (0, 0), (0, 0))
    kb = jnp.pad(k, pad).reshape(bsz, nblk, MOBA_BLOCK, heads, dh).transpose(0, 3, 1, 2, 4)
    vb = jnp.pad(v, pad).reshape(bsz, nblk, MOBA_BLOCK, heads, dh).transpose(0, 3, 1, 2, 4)
    kmean = jnp.mean(kb.astype(jnp.float32), axis=3)
    n_sel = min(MOBA_TOPK, nblk - 1)
    qb = math.gcd(MOBA_QBLOCK, lq)
    nq = lq // qb
    qs = q.reshape(bsz, nq, qb, heads, dh).transpose(1, 0, 3, 2, 4)
    ps = q_pos.reshape(nq, qb)
    scale = dh ** -0.5
    offs = jnp.arange(MOBA_BLOCK, dtype=jnp.int32)
    bi = jnp.arange(bsz)[:, None, None, None]
    hi = jnp.arange(heads)[None, :, None, None]
    blk_ids = jnp.arange(nblk, dtype=jnp.int32)

    def one(args):
        qc, pc = args
        qf = qc.astype(jnp.float32)
        own = pc[0] // MOBA_BLOCK
        k_own = lax.dynamic_index_in_dim(kb, own, axis=2, keepdims=False).astype(jnp.float32)
        v_own = lax.dynamic_index_in_dim(vb, own, axis=2, keepdims=False).astype(jnp.float32)
        own_pos = own * MOBA_BLOCK + offs
        dist_own = (pc[:, None] - own_pos[None, :]).astype(jnp.float32)
        s_own = jnp.einsum('bhqd,bhsd->bhqs', qf, k_own) * scale - slopes[None, :, None, None] * dist_own[None, None]
        s_own = jnp.where((dist_own >= 0)[None, None], s_own, -jnp.inf)
        if n_sel == 0:
            p = jax.nn.softmax(s_own, axis=-1)
            return jnp.einsum('bhqs,bhsd->bhqd', p, v_own).astype(q.dtype)
        gate = jnp.einsum('bhqd,bhnd->bhqn', qf, kmean)
        fully_past = blk_ids[None, :] < (pc // MOBA_BLOCK)[:, None]
        gate = jnp.where(fully_past[None, None], gate, -jnp.inf)
        top_v, top_i = lax.top_k(gate, n_sel)
        k_sel = kb[bi, hi, top_i].astype(jnp.float32)
        v_sel = vb[bi, hi, top_i].astype(jnp.float32)
        sel_pos = top_i[..., None] * MOBA_BLOCK + offs
        dist_sel = (pc[None, None, :, None, None] - sel_pos).astype(jnp.float32)
        s_sel = jnp.einsum('bhqd,bhqnsd->bhqns', qf, k_sel) * scale - slopes[None, :, None, None, None] * dist_sel
        s_sel = jnp.where(jnp.isfinite(top_v)[..., None], s_sel, -jnp.inf)
        m_sel = n_sel * MOBA_BLOCK
        p = jax.nn.softmax(jnp.concatenate([s_sel.reshape(bsz, heads, qb, m_sel), s_own], axis=-1), axis=-1)
        o = jnp.einsum('bhqns,bhqnsd->bhqd', p[..., :m_sel].reshape(bsz, heads, qb, n_sel, MOBA_BLOCK), v_sel)
        o = o + jnp.einsum('bhqs,bhsd->bhqd', p[..., m_sel:], v_own)
        return o.astype(q.dtype)

    o = lax.map(one, (qs, ps))
    return o.transpose(1, 0, 3, 2, 4).reshape(bsz, lq, heads, dh)


def even_mixer(xn, past_k, past_v, s0, q_pos, lb, w_in, w_out, g_hgrn, g_q, g_k, slopes):
    u = xn @ w_in
    qa, fa, ia, ga, qb_, kb_, vb_ = jnp.split(
        u, [MIX_A, 2 * MIX_A, 3 * MIX_A, 4 * MIX_A, 4 * MIX_A + MIX_B, 4 * MIX_A + 2 * MIX_B], axis=-1)
    oa, s_new = hgrn2_branch(qa, fa, ia, ga, lb, s0, g_hgrn)
    q = rmsnorm(split_heads(qb_, MOBA_HEADS), g_q)
    k = rmsnorm(split_heads(kb_, MOBA_HEADS), g_k)
    v = split_heads(vb_, MOBA_HEADS)
    if past_k is None:
        k_all, v_all = k, v
    else:
        k_all = jnp.concatenate([past_k.astype(k.dtype), k], axis=1)
        v_all = jnp.concatenate([past_v.astype(v.dtype), v], axis=1)
    ob = moba_attention(q, k_all, v_all, q_pos, slopes)
    ob = ob.reshape(ob.shape[:2] + (MIX_B,))
    y = jnp.concatenate([oa, ob.astype(oa.dtype)], axis=-1) @ w_out
    return y, k, v, s_new


def rwkv7_mixer(xn, shift, s0, mu, w_rkv, w_out, w0, w_la, w_lb, a0, a_la, a_lb, g_la, g_lb,
                k_k, k_a, r_k, ln_w, ln_b):
    bsz, length, d = xn.shape
    x_prev = jnp.concatenate([shift[:, None, :].astype(xn.dtype), xn[:, :-1]], axis=1)
    xx = x_prev - xn
    xr, xw, xk, xv, xa, xg = (xn + xx * mu[j] for j in range(6))
    r = xr @ w_rkv[0]
    k = xk @ w_rkv[1]
    v = xv @ w_rkv[2]
    w = -jax.nn.softplus(-(w0 + jnp.tanh(xw @ w_la) @ w_lb).astype(jnp.float32)) - 0.5
    decay = jnp.exp(-jnp.exp(w))
    a = jax.nn.sigmoid((a0 + (xa @ a_la) @ a_lb).astype(jnp.float32))
    g = (jax.nn.sigmoid(xg @ g_la) @ g_lb).astype(jnp.float32)
    H = RWKV_HEADS
    rh = split_heads(r.astype(jnp.float32), H)
    kh = split_heads(k.astype(jnp.float32), H)
    vh = split_heads(v.astype(jnp.float32), H)
    ah = split_heads(a, H)
    wh = split_heads(decay, H)
    kk = kh * split_heads(k_k.astype(jnp.float32), H)
    kk = kk / jnp.maximum(jnp.sqrt(jnp.sum(kk * kk, axis=-1, keepdims=True)), 1e-12)
    kh = kh * (1.0 + (ah - 1.0) * split_heads(k_a.astype(jnp.float32), H))

    def step(S, inp):
        r_t, w_t, k_t, v_t, kk_t, a_t = inp
        sa = jnp.einsum('bhvk,bhk->bhv', S, -kk_t)
        S = S * w_t[:, :, None, :] + sa[..., None] * (kk_t * a_t)[:, :, None, :] + v_t[..., None] * k_t[:, :, None, :]
        return S, jnp.einsum('bhvk,bhk->bhv', S, r_t)

    tm = lambda t: jnp.moveaxis(t, 1, 0)
    s_new, y = lax.scan(step, s0.astype(jnp.float32), (tm(rh), tm(wh), tm(kh), tm(vh), tm(kk), tm(ah)))
    y = jnp.moveaxis(y, 0, 1)
    mean = jnp.mean(y, axis=-1, keepdims=True)
    var = jnp.mean(jnp.square(y - mean), axis=-1, keepdims=True)
    y = ((y - mean) * lax.rsqrt(var + RWKV_GN_EPS)).reshape(bsz, length, d) * ln_w.astype(jnp.float32) + ln_b.astype(jnp.float32)
    bonus = (jnp.sum(rh * kh * r_k.astype(jnp.float32), axis=-1, keepdims=True) * vh).reshape(bsz, length, d)
    out = ((y + bonus) * g).astype(xn.dtype) @ w_out
    return out, s_new, xn[:, -1]


def mem_kv(mem, g_src, w_k, w_v, g_k):
    mn = rmsnorm(mem, g_src)
    k = rmsnorm(split_heads(mn @ w_k, MEM_HEADS), g_k)
    v = split_heads(mn @ w_v, MEM_HEADS)
    return k, v


def mem_attend(xn, mk, mv, w_q, g_q, w_o):
    q = rmsnorm(split_heads(xn @ w_q, MEM_HEADS), g_q)
    s = jnp.einsum('blhd,bmhd->bhlm', q.astype(jnp.float32), mk.astype(jnp.float32)) * HEAD_DIM ** -0.5
    p = jax.nn.softmax(s, axis=-1)
    o = jnp.einsum('bhlm,bmhd->blhd', p, mv.astype(jnp.float32))
    return o.reshape(o.shape[:2] + (MEM_WIDTH,)).astype(xn.dtype) @ w_o


def expert_dispatch(x, eid, gates, w_gate, w_up, w_down):
    m, d = x.shape
    kk = eid.shape[1]
    e = w_gate.shape[0]
    a = m * kk
    t = max(MOE_BLOCK_MIN, min(MOE_BLOCK_MAX, a // e))
    nb = -(-(a + e * (t - 1)) // t)
    fe = eid.reshape(-1).astype(jnp.int32)
    ft = jnp.repeat(jnp.arange(m, dtype=jnp.int32), kk)
    fg = gates.reshape(-1).astype(jnp.float32)
    order = jnp.argsort(fe)
    se, st, sg = fe[order], ft[order], fg[order]
    counts = jnp.bincount(fe, length=e)
    padded = (counts + t - 1) // t * t
    start = jnp.cumsum(counts) - counts
    pend = jnp.cumsum(padded)
    pstart = pend - padded
    dest = pstart[se] + (jnp.arange(a, dtype=jnp.int32) - start[se])
    slot_tok = jnp.zeros((nb * t,), jnp.int32).at[dest].set(st)
    slot_gate = jnp.zeros((nb * t,), jnp.float32).at[dest].set(sg)
    block_e = jnp.minimum(jnp.searchsorted(pend, jnp.arange(nb, dtype=jnp.int32) * t, side='right'), e - 1)
    xs = x[slot_tok].reshape(nb, t, d)

    def run(args):
        xb, ei = args
        h = jax.nn.silu(xb @ w_gate[ei]) * (xb @ w_up[ei])
        return h @ w_down[ei]

    ys = lax.map(run, (xs, block_e)).reshape(nb * t, d)
    out = jnp.zeros((m, d), jnp.float32).at[slot_tok].add(ys.astype(jnp.float32) * slot_gate[:, None])
    return out.astype(x.dtype)


def hier_moe(xn, w_group, w_expert, w_gate, w_up, w_down):
    bsz, length, d = xn.shape
    x = xn.reshape(bsz * length, d)
    gl = (x @ w_group).astype(jnp.float32)
    gp = jax.nn.softmax(gl, axis=-1)
    gsel = jnp.argmax(gl, axis=-1)
    gw = jnp.take_along_axis(gp, gsel[:, None], axis=-1)
    el = (x @ w_expert).astype(jnp.float32).reshape(-1, MOE_GROUPS, MOE_PER_GROUP)
    el = jnp.take_along_axis(el, gsel[:, None, None], axis=1)[:, 0]
    tv, ti = lax.top_k(el, MOE_TOPK)
    gates = gw * jax.nn.softmax(tv, axis=-1)
    eid = gsel[:, None].astype(jnp.int32) * MOE_PER_GROUP + ti.astype(jnp.int32)
    y = expert_dispatch(x, eid, gates, w_gate, w_up, w_down)
    return y.reshape(bsz, length, d)


def setup_inputs(seed: int = 0) -> dict:
    key = jax.random.key(seed)
    kit = iter(jax.random.split(key, 64))

    def nrm(shape, scale):
        return jax.random.normal(next(kit), shape, jnp.float32) * scale

    def gain(shape):
        return 1.0 + nrm(shape, 0.02)

    def unif(shape, lo, hi):
        return jax.random.uniform(next(kit), shape, jnp.float32, lo, hi)

    D = D_MODEL
    E = MOE_EXPERTS
    n_pages = PAST_LEN // PAGE_SIZE
    n_used = DEC_BATCH * n_pages
    n_pool = n_used + n_used // 4
    inputs = {
        'x_prompt': nrm((BATCH, SEQ, D), 1.0),
        'x_sample': nrm((DEC_BATCH, DEC_SEQ, D), 1.0),
        'mem_prompt': nrm((BATCH, MEM_TOKENS, D), 1.0),
        'cache_moba_k': nrm((N_EVEN, n_pool, PAGE_SIZE, MOBA_HEADS, HEAD_DIM), 1.0),
        'cache_moba_v': nrm((N_EVEN, n_pool, PAGE_SIZE, MOBA_HEADS, HEAD_DIM), 1.0),
        'page_table': jax.random.permutation(next(kit), n_pool)[:n_used].reshape(DEC_BATCH, n_pages).astype(jnp.int32),
        'state_hgrn': nrm((N_EVEN, DEC_BATCH, HGRN_HEADS, HEAD_DIM, HEAD_DIM), 0.3),
        'state_rwkv': nrm((N_ODD, DEC_BATCH, RWKV_HEADS, RWKV_HEAD, RWKV_HEAD), 0.3),
        'state_rwkv_shift': nrm((N_ODD, DEC_BATCH, D), 1.0),
        'cache_mem_k': nrm((DEPTH, DEC_BATCH, MEM_TOKENS, MEM_HEADS, HEAD_DIM), 1.0),
        'cache_mem_v': nrm((DEPTH, DEC_BATCH, MEM_TOKENS, MEM_HEADS, HEAD_DIM), 1.0),
        'norm_mix': gain((DEPTH, D)),
        'norm_mem_q': gain((DEPTH, D)),
        'norm_mem_src': gain((DEPTH, D)),
        'norm_ffn': gain((DEPTH, D)),
        'w_even_in': nrm((N_EVEN, D, IN_EVEN), D ** -0.5),
        'w_even_out': nrm((N_EVEN, MIX_A + MIX_B, D), (MIX_A + MIX_B) ** -0.5),
        'hgrn_lb_logits': nrm((N_EVEN + 1, MIX_A), 0.5),
        'hgrn_out_norm': gain((N_EVEN, HEAD_DIM)),
        'moba_q_norm': gain((N_EVEN, HEAD_DIM)),
        'moba_k_norm': gain((N_EVEN, HEAD_DIM)),
        'rwkv_mu': unif((N_ODD, 6, D), 0.0, 1.0),
        'rwkv_w_rkv': nrm((N_ODD, 3, D, D), D ** -0.5),
        'rwkv_w_out': nrm((N_ODD, D, D), D ** -0.5),
        'rwkv_w0': unif((N_ODD, D), -4.0, 1.0),
        'rwkv_w_lora_a': nrm((N_ODD, D, RWKV_DECAY_LORA), D ** -0.5),
        'rwkv_w_lora_b': nrm((N_ODD, RWKV_DECAY_LORA, D), 0.1 * RWKV_DECAY_LORA ** -0.5),
        'rwkv_a0': nrm((N_ODD, D), 0.1),
        'rwkv_a_lora_a': nrm((N_ODD, D, RWKV_AAA_LORA), D ** -0.5),
        'rwkv_a_lora_b': nrm((N_ODD, RWKV_AAA_LORA, D), 0.1 * RWKV_AAA_LORA ** -0.5),
        'rwkv_g_lora_a': nrm((N_ODD, D, RWKV_GATE_LORA), D ** -0.5),
        'rwkv_g_lora_b': nrm((N_ODD, RWKV_GATE_LORA, D), RWKV_GATE_LORA ** -0.5),
        'rwkv_k_k': 0.85 + nrm((N_ODD, D), 0.02),
        'rwkv_k_a': gain((N_ODD, D)),
        'rwkv_r_k': nrm((N_ODD, RWKV_HEADS, RWKV_HEAD), 0.1),
        'rwkv_ln_w': gain((N_ODD, D)),
        'rwkv_ln_b': nrm((N_ODD, D), 0.02),
        'mem_w_q': nrm((DEPTH, D, MEM_WIDTH), D ** -0.5),
        'mem_w_k': nrm((DEPTH, D, MEM_WIDTH), D ** -0.5),
        'mem_w_v': nrm((DEPTH, D, MEM_WIDTH), D ** -0.5),
        'mem_w_o': nrm((DEPTH, MEM_WIDTH, D), MEM_WIDTH ** -0.5),
        'mem_q_norm': gain((DEPTH, HEAD_DIM)),
        'mem_k_norm': gain((DEPTH, HEAD_DIM)),
        'moe_w_group': nrm((DEPTH, D, MOE_GROUPS), D ** -0.5),
        'moe_w_expert': nrm((DEPTH, D, E), D ** -0.5),
        'moe_w_gate': nrm((DEPTH, E, D, MOE_HIDDEN), D ** -0.5),
        'moe_w_up': nrm((DEPTH, E, D, MOE_HIDDEN), D ** -0.5),
        'moe_w_down': nrm((DEPTH, E, MOE_HIDDEN, D), MOE_HIDDEN ** -0.5),
    }
    return inputs


def reference(x_prompt, x_sample, mem_prompt, cache_moba_k, cache_moba_v, page_table, state_hgrn,
              state_rwkv, state_rwkv_shift, cache_mem_k, cache_mem_v, norm_mix, norm_mem_q, norm_mem_src,
              norm_ffn, w_even_in, w_even_out, hgrn_lb_logits, hgrn_out_norm, moba_q_norm, moba_k_norm,
              rwkv_mu, rwkv_w_rkv, rwkv_w_out, rwkv_w0, rwkv_w_lora_a, rwkv_w_lora_b, rwkv_a0,
              rwkv_a_lora_a, rwkv_a_lora_b, rwkv_g_lora_a, rwkv_g_lora_b, rwkv_k_k, rwkv_k_a, rwkv_r_k,
              rwkv_ln_w, rwkv_ln_b, mem_w_q, mem_w_k, mem_w_v, mem_w_o, mem_q_norm, mem_k_norm,
              moe_w_group, moe_w_expert, moe_w_gate, moe_w_up, moe_w_down):
    slopes = alibi_slopes(MOBA_HEADS)
    lbs = jnp.cumsum(jax.nn.softmax(hgrn_lb_logits.astype(jnp.float32), axis=0), axis=0)
    b_p, l_p, _ = x_prompt.shape
    b_s, l_s, _ = x_sample.shape
    past = page_table.shape[1] * cache_moba_k.shape[2]
    pos_p = jnp.arange(l_p, dtype=jnp.int32)
    pos_s = past + jnp.arange(l_s, dtype=jnp.int32)
    hp, hs = x_prompt, x_sample
    kp_l, vp_l, ks_l, vs_l, hgp_l, hgs_l = [], [], [], [], [], []
    rwp_l, rws_l, shp_l, shs_l, mkp_l, mvp_l = [], [], [], [], [], []
    for l in range(DEPTH):
        i = l // 2
        xpn = rmsnorm(hp, norm_mix[l])
        xsn = rmsnorm(hs, norm_mix[l])
        if l % 2 == 0:
            rows = (b_s, past, MOBA_HEADS, HEAD_DIM)
            pk = cache_moba_k[i, page_table].reshape(rows)
            pv = cache_moba_v[i, page_table].reshape(rows)
            wargs = (lbs[i], w_even_in[i], w_even_out[i], hgrn_out_norm[i], moba_q_norm[i], moba_k_norm[i], slopes)
            s0p = jnp.zeros((b_p, HGRN_HEADS, HEAD_DIM, HEAD_DIM), jnp.float32)
            yp, kp, vp, sp = even_mixer(xpn, None, None, s0p, pos_p, *wargs)
            ys, ks, vs, ss = even_mixer(xsn, pk, pv, state_hgrn[i], pos_s, *wargs)
            kp_l.append(kp)
            vp_l.append(vp)
            ks_l.append(ks)
            vs_l.append(vs)
            hgp_l.append(sp)
            hgs_l.append(ss)
        else:
            wargs = (rwkv_mu[i], rwkv_w_rkv[i], rwkv_w_out[i], rwkv_w0[i], rwkv_w_lora_a[i], rwkv_w_lora_b[i],
                     rwkv_a0[i], rwkv_a_lora_a[i], rwkv_a_lora_b[i], rwkv_g_lora_a[i], rwkv_g_lora_b[i],
                     rwkv_k_k[i], rwkv_k_a[i], rwkv_r_k[i], rwkv_ln_w[i], rwkv_ln_b[i])
            shift0 = jnp.zeros((b_p, D_MODEL), x_prompt.dtype)
            s0p = jnp.zeros((b_p, RWKV_HEADS, RWKV_HEAD, RWKV_HEAD), jnp.float32)
            yp, sp, shp = rwkv7_mixer(xpn, shift0, s0p, *wargs)
            ys, ss, shs = rwkv7_mixer(xsn, state_rwkv_shift[i], state_rwkv[i], *wargs)
            rwp_l.append(sp)
            rws_l.append(ss)
            shp_l.append(shp)
            shs_l.append(shs)
        hp = hp + yp
        hs = hs + ys
        mk, mv = mem_kv(mem_prompt, norm_mem_src[l], mem_w_k[l], mem_w_v[l], mem_k_norm[l])
        mkp_l.append(mk)
        mvp_l.append(mv)
        hp = hp + mem_attend(rmsnorm(hp, norm_mem_q[l]), mk, mv, mem_w_q[l], mem_q_norm[l], mem_w_o[l])
        hs = hs + mem_attend(rmsnorm(hs, norm_mem_q[l]), cache_mem_k[l], cache_mem_v[l], mem_w_q[l], mem_q_norm[l], mem_w_o[l])
        wg, we = moe_w_group[l], moe_w_expert[l]
        w1, w3, w2 = moe_w_gate[l], moe_w_up[l], moe_w_down[l]
        hp = hp + hier_moe(rmsnorm(hp, norm_ffn[l]), wg, we, w1, w3, w2)
        hs = hs + hier_moe(rmsnorm(hs, norm_ffn[l]), wg, we, w1, w3, w2)
    prompt_moba_k = jnp.stack(kp_l)
    prompt_moba_v = jnp.stack(vp_l)
    prompt_hgrn = jnp.stack(hgp_l)
    prompt_rwkv = jnp.stack(rwp_l)
    prompt_rwkv_shift = jnp.stack(shp_l)
    prompt_mem_k = jnp.stack(mkp_l)
    prompt_mem_v = jnp.stack(mvp_l)
    sample_moba_k = jnp.stack(ks_l)
    sample_moba_v = jnp.stack(vs_l)
    sample_hgrn = jnp.stack(hgs_l)
    sample_rwkv = jnp.stack(rws_l)
    sample_rwkv_shift = jnp.stack(shs_l)
    return (hp, hs, prompt_moba_k, prompt_moba_v, prompt_hgrn, prompt_rwkv, prompt_rwkv_shift,
            prompt_mem_k, prompt_mem_v, sample_moba_k, sample_moba_v, sample_hgrn, sample_rwkv,
            sample_rwkv_shift)
```

```python
import functools
import math

import jax
import jax.numpy as jnp
from jax import lax
from jax.experimental import pallas as pl
from jax.experimental.pallas import tpu as pltpu

F32 = jnp.float32
BF16 = jnp.bfloat16
I32 = jnp.int32
HIGHEST = lax.Precision.HIGHEST

D = 4096
B_P, L_P = 4, 2048
B_S, L_S = 8, 8
M_P = B_P * L_P
M_S = B_S * L_S
M = M_P + M_S
HD = 128
MIX = D // 2
N_MIX_HEADS = MIX // HD
MOBA_BLOCK = 256
MOBA_TOPK = 3
PAGE = 128
N_PAGES = 128
PAST = N_PAGES * PAGE
N_PAST_BLOCKS = PAST // MOBA_BLOCK
RW_HEAD = 64
RW_PAIRS = D // (2 * RW_HEAD)
RW_CHUNK = 64
RW_GN_EPS = 64e-5
MEM_T = 256
MEM_HEADS = 4
MEM_W = MEM_HEADS * HD
N_GROUPS = 8
PER_GROUP = 8
N_EXPERTS = N_GROUPS * PER_GROUP
MOE_HIDDEN = D // 4
NORM_EPS = 1e-6
NEG = -1e30

TM = 688
TM_NORM = 344
MOE_T = 384
MOE_NB = -(-(2 * M + N_EXPERTS * (MOE_T - 1)) // MOE_T)
MOE_HC = 128

NT_DIMS = (((1,), (1,)), ((), ()))
TN_DIMS = (((0,), (0,)), ((), ()))


def _cparams(sem, vmem_mb=48):
    return pltpu.CompilerParams(dimension_semantics=sem, vmem_limit_bytes=vmem_mb * 1024 * 1024)


def _sigmoid(x):
    return jax.nn.sigmoid(x)


def _rms_body(x_ref, g_ref, o_ref):
    x = x_ref[...]
    y = x * lax.rsqrt(jnp.mean(x * x, axis=-1, keepdims=True) + NORM_EPS)
    o_ref[...] = (y * g_ref[...]).astype(o_ref.dtype)


def _rms_router_body(x_ref, g_ref, wr_ref, o_ref, lg_ref):
    x = x_ref[...]
    y = x * lax.rsqrt(jnp.mean(x * x, axis=-1, keepdims=True) + NORM_EPS) * g_ref[...]
    o_ref[...] = y.astype(o_ref.dtype)
    lg_ref[...] = jnp.dot(y, wr_ref[...], precision=HIGHEST, preferred_element_type=F32)


def _rmsnorm(x, g, tm, w_router=None):
    m, d = x.shape
    g = g.reshape(1, d)
    row = pl.BlockSpec((tm, d), lambda i: (i, 0))
    par = pl.BlockSpec((1, d), lambda i: (0, 0))
    if w_router is None:
        return pl.pallas_call(
            _rms_body, grid=(m // tm,), in_specs=[row, par], out_specs=row,
            out_shape=jax.ShapeDtypeStruct((m, d), BF16),
            compiler_params=_cparams(("parallel",)), name="rmsnorm")(x, g)
    nr = w_router.shape[1]
    return pl.pallas_call(
        _rms_router_body, grid=(m // tm,),
        in_specs=[row, par, pl.BlockSpec((d, nr), lambda i: (0, 0))],
        out_specs=[row, pl.BlockSpec((tm, nr), lambda i: (i, 0))],
        out_shape=[jax.ShapeDtypeStruct((m, d), BF16), jax.ShapeDtypeStruct((m, nr), F32)],
        compiler_params=_cparams(("parallel",)), name="rmsnorm_router")(x, g, w_router)


def _mm_body(*refs, n_rows, n_tiles, epilogue):
    a_ref, w_ref = refs[:2]
    rows = refs[2:2 + n_rows]
    tiles = refs[2 + n_rows:2 + n_rows + n_tiles]
    o_ref = refs[2 + n_rows + n_tiles]
    wb_ref = refs[3 + n_rows + n_tiles]

    @pl.when(pl.program_id(1) == 0)
    def _():
        wb_ref[...] = w_ref[...].astype(BF16)

    acc = jnp.dot(a_ref[...], wb_ref[...], preferred_element_type=F32)
    if epilogue is not None:
        acc = epilogue(acc, *[r[...] for r in rows], *[t[...] for t in tiles])
    o_ref[...] = acc.astype(o_ref.dtype)


def _matmul(a, w, *, m=M, a_row0=0, col0=0, n=None, tm=TM, tn=512, rows=(), tiles=(),
            epilogue=None, out_dtype=F32, name="matmul"):
    k = a.shape[1]
    n = w.shape[1] if n is None else n
    assert m % tm == 0 and n % tn == 0 and a_row0 % tm == 0 and col0 % tn == 0 and w.shape[0] == k
    i0, j0 = a_row0 // tm, col0 // tn
    in_specs = [pl.BlockSpec((tm, k), lambda j, i: (i + i0, 0)),
                pl.BlockSpec((k, tn), lambda j, i: (0, j + j0))]
    in_specs += [pl.BlockSpec((1, tn), lambda j, i: (0, j)) for _ in rows]
    in_specs += [pl.BlockSpec((tm, tn), lambda j, i: (i, j)) for _ in tiles]
    return pl.pallas_call(
        functools.partial(_mm_body, n_rows=len(rows), n_tiles=len(tiles), epilogue=epilogue),
        grid=(n // tn, m // tm), in_specs=in_specs,
        out_specs=pl.BlockSpec((tm, tn), lambda j, i: (i, j)),
        out_shape=jax.ShapeDtypeStruct((m, n), out_dtype),
        scratch_shapes=[pltpu.VMEM((k, tn), BF16)],
        compiler_params=_cparams(("parallel", "arbitrary")), name=name,
    )(a, w, *[r.reshape(1, n) for r in rows], *tiles)


def _ep_headnorm(acc, g):
    outs = []
    for h in range(acc.shape[1] // HD):
        x = acc[:, h * HD:(h + 1) * HD]
        outs.append(x * lax.rsqrt(jnp.mean(x * x, axis=-1, keepdims=True) + NORM_EPS) * g[:, h * HD:(h + 1) * HD])
    return jnp.concatenate(outs, axis=1)


def _ep_residual(acc, res):
    return res + acc


def _ep_tanh(acc):
    return jnp.tanh(acc)


def _ep_sigmoid(acc):
    return _sigmoid(acc)


def _ep_bias_sigmoid(acc, bias):
    return _sigmoid(acc + bias)


def _ep_log_decay(acc, w0):
    z = -(w0 + acc)
    softplus = jnp.maximum(z, 0.0) + jnp.log(1.0 + jnp.exp(-jnp.abs(z)))
    return -jnp.exp(-softplus - 0.5)


def _hgrn_body(*refs, C, Lc, hb, has_s0):
    q_ref, f_ref, i_ref, g_ref, lb_ref, gn_ref = refs[:6]
    if has_s0:
        s0_ref, o_ref, sf_ref, st_ref, sq, sb, sk = refs[6:]
    else:
        o_ref, sf_ref, st_ref, sq, sb, sk = refs[6:]
    c = pl.program_id(2)

    @pl.when(c == 0)
    def _():
        for h in range(hb):
            st_ref[h] = s0_ref[0, h].T if has_s0 else jnp.zeros((HD, HD), F32)

    rl = lax.broadcasted_iota(I32, (Lc, Lc), 0)
    cl = lax.broadcasted_iota(I32, (Lc, Lc), 1)
    tri = ((rl >= cl) & ((rl & -C) == (cl & -C))).astype(F32)
    for h in range(hb):
        sl = slice(h * HD, (h + 1) * HD)
        qa = q_ref[:, sl]
        lb = lb_ref[:, sl]
        f = lb + (1.0 - lb) * _sigmoid(f_ref[:, sl])
        sq[h] = qa * _sigmoid(qa) * (HD ** -0.5)
        sk[h] = 1.0 - f
        sb[h] = jnp.dot(tri, jnp.log(f), precision=HIGHEST, preferred_element_type=F32)

    rowc = lax.broadcasted_iota(I32, (C, 1), 0)
    gn = gn_ref[...]

    def sub(j, carry):
        r0 = pl.multiple_of(j * C, C)
        for h in range(hb):
            sl = slice(h * HD, (h + 1) * HD)
            q = sq[h, pl.ds(r0, C), :]
            b = sb[h, pl.ds(r0, C), :]
            k = sk[h, pl.ds(r0, C), :]
            v = i_ref[pl.ds(r0, C), sl]
            st = st_ref[h]
            o = lax.dot_general((q * jnp.exp(b)).astype(BF16), st.astype(BF16), NT_DIMS,
                                preferred_element_type=F32)
            for s in range(C):
                e = jnp.exp(jnp.minimum(b - b[s:s + 1, :], 0.0))
                att = jnp.sum(q * e * k[s:s + 1, :], axis=-1, keepdims=True)
                o = o + jnp.where(rowc >= s, att, 0.0) * v[s:s + 1, :]
            blast = b[C - 1:C, :]
            kd = k * jnp.exp(blast - b)
            st_ref[h] = st * jnp.exp(blast) + lax.dot_general(
                v.astype(BF16), kd.astype(BF16), TN_DIMS, preferred_element_type=F32)
            ga = g_ref[pl.ds(r0, C), sl]
            on = o * lax.rsqrt(jnp.mean(o * o, axis=-1, keepdims=True) + NORM_EPS) * gn
            o_ref[pl.ds(r0, C), sl] = (on * (ga * _sigmoid(ga))).astype(o_ref.dtype)
        return carry

    lax.fori_loop(0, Lc // C, sub, 0)

    @pl.when(c == pl.num_programs(2) - 1)
    def _():
        for h in range(hb):
            sf_ref[0, h] = st_ref[h].T


def _hgrn(u, lb, g_out, s0, *, bsz, length, row0, C, Lc, hb=4):
    nlc = length // Lc
    rb0 = row0 // Lc
    wb = hb * HD
    ncol = MIX // wb

    def col(seg):
        return pl.BlockSpec((Lc, wb), lambda b, hg, c: (rb0 + b * nlc + c, seg * ncol + hg))

    in_specs = [col(0), col(1), col(2), col(3),
                pl.BlockSpec((1, wb), lambda b, hg, c: (0, hg)),
                pl.BlockSpec((1, HD), lambda b, hg, c: (0, 0))]
    args = [u, u, u, u, lb.reshape(1, MIX), g_out.reshape(1, HD)]
    if s0 is not None:
        in_specs.append(pl.BlockSpec((1, hb, HD, HD), lambda b, hg, c: (b, hg, 0, 0)))
        args.append(s0)
    return pl.pallas_call(
        functools.partial(_hgrn_body, C=C, Lc=Lc, hb=hb, has_s0=s0 is not None),
        grid=(bsz, ncol, nlc), in_specs=in_specs,
        out_specs=[pl.BlockSpec((Lc, wb), lambda b, hg, c: (b * nlc + c, hg)),
                   pl.BlockSpec((1, hb, HD, HD), lambda b, hg, c: (b, hg, 0, 0))],
        out_shape=[jax.ShapeDtypeStruct((bsz * length, MIX), BF16),
                   jax.ShapeDtypeStruct((bsz, N_MIX_HEADS, HD, HD), F32)],
        scratch_shapes=[pltpu.VMEM((hb, HD, HD), F32)] + [pltpu.VMEM((hb, Lc, HD), F32)] * 3,
        compiler_params=_cparams(("parallel", "parallel", "arbitrary")), name="hgrn2",
    )(*args)


def _moba_p_body(q_ref, k_ref, v_ref, sl_ref, o_ref, km_ref, *, tq):
    i = pl.program_id(2)
    own = (i * tq) // MOBA_BLOCK
    nblk = L_P // MOBA_BLOCK

    @pl.when(i == 0)
    def _():
        km_ref[...] = jnp.zeros((HD, HD), F32)
        for j in range(nblk):
            km_ref[j:j + 1, :] = jnp.mean(k_ref[j * MOBA_BLOCK:(j + 1) * MOBA_BLOCK, :], axis=0, keepdims=True)

    q = q_ref[...]
    gate = lax.dot_general(q, km_ref[...], NT_DIMS, precision=HIGHEST, preferred_element_type=F32)
    lane = lax.broadcasted_iota(I32, (tq, HD), 1)
    past = lane < own
    selmask = jnp.zeros((tq, HD), F32)
    for j in range(nblk - 1):
        gj = gate[:, j:j + 1]
        beats = past & ((gate > gj) | ((gate == gj) & (lane < j)))
        cnt = jnp.sum(beats.astype(F32), axis=-1, keepdims=True)
        selmask = jnp.where((lane == j) & past, (cnt < MOBA_TOPK).astype(F32), selmask)

    slope = sl_ref[0, :, 0:1]
    scale = HD ** -0.5
    qb = q.astype(BF16)
    row = lax.broadcasted_iota(I32, (tq, MOBA_BLOCK), 0)
    colk = lax.broadcasted_iota(I32, (tq, MOBA_BLOCK), 1)

    def block(j, carry, msk_fn):
        m, l, acc = carry
        r0 = pl.multiple_of(j * MOBA_BLOCK, MOBA_BLOCK)
        kj = k_ref[pl.ds(r0, MOBA_BLOCK), :].astype(BF16)
        vj = v_ref[pl.ds(r0, MOBA_BLOCK), :].astype(BF16)
        dist = (i * tq - j * MOBA_BLOCK) + row - colk
        s = lax.dot_general(qb, kj, NT_DIMS, preferred_element_type=F32) * scale - slope * dist.astype(F32)
        msk = msk_fn(j, dist)
        s = jnp.where(msk, s, NEG)
        m_new = jnp.maximum(m, jnp.max(s, axis=-1, keepdims=True))
        alpha = jnp.exp(m - m_new)
        p = jnp.where(msk, jnp.exp(s - m_new), 0.0)
        l = alpha * l + jnp.sum(p, axis=-1, keepdims=True)
        acc = alpha * acc + jnp.dot(p.astype(BF16), vj, preferred_element_type=F32)
        return m_new, l, acc

    def past_mask(j, dist):
        return jnp.sum(jnp.where(lane == j, selmask, 0.0), axis=-1, keepdims=True) > 0.5

    def own_mask(j, dist):
        return dist >= 0

    init = (jnp.full((tq, 1), NEG, F32), jnp.zeros((tq, 1), F32), jnp.zeros((tq, HD), F32))
    carry = lax.fori_loop(0, own, lambda j, cr: block(j, cr, past_mask), init)
    m, l, acc = block(own, carry, own_mask)
    o_ref[...] = (acc / l).astype(o_ref.dtype)


def _moba_prompt(q, k, v, slopes, tq=128):
    nq = L_P // tq
    return pl.pallas_call(
        functools.partial(_moba_p_body, tq=tq),
        grid=(B_P, N_MIX_HEADS, nq),
        in_specs=[pl.BlockSpec((tq, HD), lambda b, h, i: (b * nq + i, h)),
                  pl.BlockSpec((L_P, HD), lambda b, h, i: (b, h)),
                  pl.BlockSpec((L_P, HD), lambda b, h, i: (b, h)),
                  pl.BlockSpec((1, 1, HD), lambda b, h, i: (h, 0, 0))],
        out_specs=pl.BlockSpec((tq, HD), lambda b, h, i: (b * nq + i, h)),
        out_shape=jax.ShapeDtypeStruct((M_P, MIX), BF16),
        scratch_shapes=[pltpu.VMEM((HD, HD), F32)],
        compiler_params=_cparams(("parallel", "parallel", "arbitrary")), name="moba_prompt",
    )(q, k, v, slopes)


def _kmean_body(pt_ref, k_ref, o_ref):
    j = pl.program_id(1)
    half = pl.program_id(2)
    s = jnp.sum(k_ref[0], axis=0, keepdims=True)

    @pl.when(half == 0)
    def _():
        o_ref[0, pl.ds(j, 1), :] = s

    @pl.when(half == 1)
    def _():
        o_ref[0, pl.ds(j, 1), :] = (o_ref[0, pl.ds(j, 1), :] + s) * (1.0 / MOBA_BLOCK)


def _moba_kmean(pt_flat, cache_k):
    grid_spec = pltpu.PrefetchScalarGridSpec(
        num_scalar_prefetch=1, grid=(B_S, N_PAST_BLOCKS, 2),
        in_specs=[pl.BlockSpec((1, PAGE, MIX), lambda b, j, hf, pt: (pt[b * N_PAGES + 2 * j + hf], 0, 0))],
        out_specs=pl.BlockSpec((1, N_PAST_BLOCKS, MIX), lambda b, j, hf, pt: (b, 0, 0)))
    return pl.pallas_call(
        _kmean_body, grid_spec=grid_spec,
        out_shape=jax.ShapeDtypeStruct((B_S, N_PAST_BLOCKS, MIX), F32),
        compiler_params=_cparams(("parallel", "arbitrary", "arbitrary")), name="moba_kmean",
    )(pt_flat, cache_k)


def _moba_sel_body(q_ref, km_ref, o_ref):
    lane = lax.broadcasted_iota(I32, (L_S, N_PAST_BLOCKS), 1).astype(F32)
    lane_o = lax.broadcasted_iota(I32, (L_S, HD), 1)
    for h in range(N_MIX_HEADS):
        sl = slice(h * HD, (h + 1) * HD)
        g = lax.dot_general(q_ref[:, sl], km_ref[0, :, sl], NT_DIMS, precision=HIGHEST,
                            preferred_element_type=F32)
        res = jnp.zeros((L_S, HD), I32)
        for n in range(MOBA_TOPK):
            mx = jnp.max(g, axis=-1, keepdims=True)
            idx = jnp.min(jnp.where(g == mx, lane, float(N_PAST_BLOCKS)), axis=-1, keepdims=True)
            g = jnp.where(lane == idx, -jnp.inf, g)
            res = jnp.where(lane_o == n, idx.astype(I32), res)
        o_ref[0, h] = res


def _moba_select(q, kmean):
    rb0 = M_P // L_S
    return pl.pallas_call(
        _moba_sel_body, grid=(B_S,),
        in_specs=[pl.BlockSpec((L_S, MIX), lambda b: (rb0 + b, 0)),
                  pl.BlockSpec((1, N_PAST_BLOCKS, MIX), lambda b: (b, 0, 0))],
        out_specs=pl.BlockSpec((1, N_MIX_HEADS, L_S, HD), lambda b: (b, 0, 0, 0)),
        out_shape=jax.ShapeDtypeStruct((B_S, N_MIX_HEADS, L_S, HD), I32),
        compiler_params=_cparams(("parallel",)), name="moba_select",
    )(q, kmean)


def _moba_s_body(pt_ref, sel_ref, q_ref, ka_ref, kb_ref, va_ref, vb_ref, kn_ref, vn_ref, sl_ref,
                 o_ref, m_ref, l_ref, acc_ref):
    b, h, t, n = pl.program_id(0), pl.program_id(1), pl.program_id(2), pl.program_id(3)

    @pl.when((t == 0) & (n == 0))
    def _():
        m_ref[...] = jnp.full((L_S, 1), NEG, F32)
        l_ref[...] = jnp.zeros((L_S, 1), F32)
        acc_ref[...] = jnp.zeros((L_S, HD), F32)

    blk = sel_ref[((b * N_MIX_HEADS + h) * L_S + t) * MOBA_TOPK + n]
    scale = HD ** -0.5
    slope = sl_ref[0, :, 0:1]
    qb = q_ref[...].astype(BF16)
    row = lax.broadcasted_iota(I32, (L_S, PAGE), 0)
    colk = lax.broadcasted_iota(I32, (L_S, PAGE), 1)

    def update(s, msk, v):
        m = m_ref[...]
        s = jnp.where(msk, s, NEG)
        m_new = jnp.maximum(m, jnp.max(s, axis=-1, keepdims=True))
        alpha = jnp.exp(m - m_new)
        p = jnp.where(msk, jnp.exp(s - m_new), 0.0)
        l_ref[...] = alpha * l_ref[...] + jnp.sum(p, axis=-1, keepdims=True)
        acc_ref[...] = alpha * acc_ref[...] + jnp.dot(p.astype(BF16), v.astype(BF16), preferred_element_type=F32)
        m_ref[...] = m_new

    for half, (kr, vr) in enumerate(((ka_ref, va_ref), (kb_ref, vb_ref))):
        dist = (PAST + row) - (blk * MOBA_BLOCK + half * PAGE + colk)
        s = lax.dot_general(qb, kr[0].astype(BF16), NT_DIMS, preferred_element_type=F32) * scale
        update(s - slope * dist.astype(F32), row == t, vr[0])

    @pl.when((t == L_S - 1) & (n == MOBA_TOPK - 1))
    def _():
        r8 = lax.broadcasted_iota(I32, (L_S, L_S), 0)
        c8 = lax.broadcasted_iota(I32, (L_S, L_S), 1)
        s = lax.dot_general(qb, kn_ref[...].astype(BF16), NT_DIMS, preferred_element_type=F32) * scale
        update(s - slope * (r8 - c8).astype(F32), r8 >= c8, vn_ref[...])
        o_ref[...] = (acc_ref[...] / l_ref[...]).astype(o_ref.dtype)


def _moba_sample(pt_flat, sel_flat, q, k, v, cache_k, cache_v, slopes):
    rb0 = M_P // L_S

    def page(half):
        def index(b, h, t, n, pt, sel):
            blk = sel[((b * N_MIX_HEADS + h) * L_S + t) * MOBA_TOPK + n]
            return (pt[b * N_PAGES + 2 * blk + half], 0, h)
        return pl.BlockSpec((1, PAGE, HD), index)

    new = pl.BlockSpec((L_S, HD), lambda b, h, t, n, pt, sel: (rb0 + b, h))
    grid_spec = pltpu.PrefetchScalarGridSpec(
        num_scalar_prefetch=2, grid=(B_S, N_MIX_HEADS, L_S, MOBA_TOPK),
        in_specs=[new, page(0), page(1), page(0), page(1), new, new,
                  pl.BlockSpec((1, 1, HD), lambda b, h, t, n, pt, sel: (h, 0, 0))],
        out_specs=pl.BlockSpec((L_S, HD), lambda b, h, t, n, pt, sel: (b, h)),
        scratch_shapes=[pltpu.VMEM((L_S, 1), F32), pltpu.VMEM((L_S, 1), F32), pltpu.VMEM((L_S, HD), F32)])
    return pl.pallas_call(
        _moba_s_body, grid_spec=grid_spec,
        out_shape=jax.ShapeDtypeStruct((M_S, MIX), BF16),
        compiler_params=_cparams(("parallel", "parallel", "arbitrary", "arbitrary")), name="moba_sample",
    )(pt_flat, sel_flat, q, cache_k, cache_k, cache_v, cache_v, k, v, slopes)


def _memattn_body(q_ref, k_ref, v_ref, o_ref):
    scale = HD ** -0.5
    for h in range(MEM_HEADS):
        sl = slice(h * HD, (h + 1) * HD)
        s = lax.dot_general(q_ref[:, sl].astype(BF16), k_ref[:, sl].astype(BF16), NT_DIMS,
                            preferred_element_type=F32) * scale
        p = jnp.exp(s - jnp.max(s, axis=-1, keepdims=True))
        o = jnp.dot(p.astype(BF16), v_ref[:, sl].astype(BF16), preferred_element_type=F32)
        o_ref[:, sl] = (o / jnp.sum(p, axis=-1, keepdims=True)).astype(o_ref.dtype)


def _mem_attend(q, k_all, v_all, *, n_tiles, tq, q_rb0, kv_of_tile):
    kv = pl.BlockSpec((MEM_T, MEM_W), lambda i: (kv_of_tile(i), 0))
    return pl.pallas_call(
        _memattn_body, grid=(n_tiles,),
        in_specs=[pl.BlockSpec((tq, MEM_W), lambda i: (q_rb0 + i, 0)), kv, kv],
        out_specs=pl.BlockSpec((tq, MEM_W), lambda i: (i, 0)),
        out_shape=jax.ShapeDtypeStruct((n_tiles * tq, MEM_W), BF16),
        compiler_params=_cparams(("parallel",)), name="mem_attend",
    )(q, k_all, v_all)


def _route_body(lg_ref, o_ref):
    x = lg_ref[...]
    lane = lax.broadcasted_iota(I32, x.shape, 1).astype(F32)
    big = 4.0 * HD
    gmask = lane < N_GROUPS
    gl = jnp.where(gmask, x, -jnp.inf)
    gmax = jnp.max(gl, axis=-1, keepdims=True)
    gsel = jnp.min(jnp.where(gl == gmax, lane, big), axis=-1, keepdims=True)
    gw = 1.0 / jnp.sum(jnp.where(gmask, jnp.exp(x - gmax), 0.0), axis=-1, keepdims=True)
    lo = N_GROUPS + gsel * PER_GROUP
    el = jnp.where((lane >= lo) & (lane < lo + PER_GROUP), x, -jnp.inf)
    m1 = jnp.max(el, axis=-1, keepdims=True)
    i1 = jnp.min(jnp.where(el == m1, lane, big), axis=-1, keepdims=True)
    el2 = jnp.where(lane == i1, -jnp.inf, el)
    m2 = jnp.max(el2, axis=-1, keepdims=True)
    i2 = jnp.min(jnp.where(el2 == m2, lane, big), axis=-1, keepdims=True)
    e2 = jnp.exp(m2 - m1)
    p1 = 1.0 / (1.0 + e2)
    p2 = e2 / (1.0 + e2)
    out = jnp.where(lane == 0, i1 - N_GROUPS, 0.0)
    out = jnp.where(lane == 1, i2 - N_GROUPS, out)
    out = jnp.where(lane == 2, gw * p1, out)
    out = jnp.where(lane == 3, gw * p2, out)
    o_ref[...] = out


def _route(logits, tm=TM_NORM):
    m, nr = logits.shape
    spec = pl.BlockSpec((tm, nr), lambda i: (i, 0))
    return pl.pallas_call(
        _route_body, grid=(m // tm,), in_specs=[spec], out_specs=spec,
        out_shape=jax.ShapeDtypeStruct((m, nr), F32),
        compiler_params=_cparams(("parallel",)), name="moe_route")(logits)


def _moe_body(be_ref, nv_ref, x_ref, wg_ref, wu_ref, wd_ref, o_ref):
    nb = pl.program_id(0)
    c = pl.program_id(1)

    @pl.when(nb < nv_ref[0])
    def _():
        x = x_ref[...]
        hg = jnp.dot(x, wg_ref[0].astype(BF16), preferred_element_type=F32)
        hu = jnp.dot(x, wu_ref[0].astype(BF16), preferred_element_type=F32)
        hh = (hg * _sigmoid(hg) * hu).astype(BF16)
        y = jnp.dot(hh, wd_ref[0].astype(BF16), preferred_element_type=F32)

        @pl.when(c == 0)
        def _():
            o_ref[...] = y

        @pl.when(c > 0)
        def _():
            o_ref[...] += y


def _moe_experts(block_e, n_valid, xs, w_gate, w_up, w_down):
    grid_spec = pltpu.PrefetchScalarGridSpec(
        num_scalar_prefetch=2, grid=(MOE_NB, MOE_HIDDEN // MOE_HC),
        in_specs=[pl.BlockSpec((MOE_T, D), lambda nb, c, be, nv: (nb, 0)),
                  pl.BlockSpec((1, D, MOE_HC), lambda nb, c, be, nv: (be[nb], 0, c)),
                  pl.BlockSpec((1, D, MOE_HC), lambda nb, c, be, nv: (be[nb], 0, c)),
                  pl.BlockSpec((1, MOE_HC, D), lambda nb, c, be, nv: (be[nb], c, 0))],
        out_specs=pl.BlockSpec((MOE_T, D), lambda nb, c, be, nv: (nb, 0)))
    return pl.pallas_call(
        _moe_body, grid_spec=grid_spec,
        out_shape=jax.ShapeDtypeStruct((MOE_NB * MOE_T, D), F32),
        compiler_params=_cparams(("parallel", "arbitrary")), name="moe_experts",
    )(block_e, n_valid, xs, w_gate, w_up, w_down)


def _moe_block(h, g_norm, w_group, w_expert, w_gate, w_up, w_down):
    nr = HD
    w_router = jnp.concatenate(
        [w_group, w_expert, jnp.zeros((D, nr - N_GROUPS - N_EXPERTS), F32)], axis=1)
    xf, logits = _rmsnorm(h, g_norm, TM_NORM, w_router)
    route = _route(logits)
    eid = route[:, 0:2].astype(I32)
    gates = route[:, 2:4]
    na = 2 * M
    fe = eid.reshape(-1)
    order = jnp.argsort(fe)
    se = fe[order]
    counts = jnp.bincount(fe, length=N_EXPERTS)
    padded = (counts + MOE_T - 1) // MOE_T * MOE_T
    start = jnp.cumsum(counts) - counts
    pend = jnp.cumsum(padded)
    pstart = pend - padded
    dest_sorted = (pstart[se] + (jnp.arange(na, dtype=I32) - start[se])).astype(I32)
    dest = jnp.zeros((na,), I32).at[order].set(dest_sorted)
    slot_tok = jnp.zeros((MOE_NB * MOE_T,), I32).at[dest_sorted].set((order // 2).astype(I32))
    n_valid = (pend[-1] // MOE_T).astype(I32)
    blk = jnp.arange(MOE_NB, dtype=I32)
    block_e = jnp.minimum(jnp.searchsorted(pend, blk * MOE_T, side='right'), N_EXPERTS - 1).astype(I32)
    block_e = jnp.where(blk < n_valid, block_e, block_e[jnp.maximum(n_valid - 1, 0)])
    xs = xf[slot_tok]
    ys = _moe_experts(block_e, n_valid.reshape(1), xs, w_gate, w_up, w_down)
    y2 = ys[dest].reshape(M, 2, D) * gates[:, :, None]
    return h + (y2[:, 0] + y2[:, 1])


RW_PRO_T = 64


def _rwkv_pro_body(h_ref, g_ref, mu_ref, ovr_ref, xm_ref, xn_ref, buf):
    i = pl.program_id(0)
    x = h_ref[...]
    xn = x * lax.rsqrt(jnp.mean(x * x, axis=-1, keepdims=True) + NORM_EPS) * g_ref[...]
    xn_ref[...] = xn

    @pl.when(i == 0)
    def _():
        buf[0:8, :] = jnp.zeros((8, D), F32)

    buf[8:8 + RW_PRO_T, :] = xn
    prev = buf[7:7 + RW_PRO_T, :]
    grow = i * RW_PRO_T + lax.broadcasted_iota(I32, (RW_PRO_T, 1), 0)
    seq_start = ((grow & (L_S - 1)) == 0) & ((grow >= M_P) | ((grow & (L_P - 1)) == 0))
    prev = jnp.where(seq_start, ovr_ref[...], prev)
    xx = prev - xn
    for j in range(6):
        xm_ref[j] = (xn + xx * mu_ref[j:j + 1, :]).astype(BF16)
    buf[0:8, :] = xn[RW_PRO_T - 8:RW_PRO_T, :]


def _rwkv_prologue(h, g, mu, shift_s):
    ovr = jnp.concatenate([jnp.zeros((RW_PRO_T, D), F32), jnp.repeat(shift_s, L_S, axis=0)], axis=0)
    n_p = M_P // RW_PRO_T
    row = pl.BlockSpec((RW_PRO_T, D), lambda i: (i, 0))
    return pl.pallas_call(
        _rwkv_pro_body, grid=(M // RW_PRO_T,),
        in_specs=[row, pl.BlockSpec((1, D), lambda i: (0, 0)), pl.BlockSpec((6, D), lambda i: (0, 0)),
                  pl.BlockSpec((RW_PRO_T, D), lambda i: (i // n_p, 0))],
        out_specs=[pl.BlockSpec((6, RW_PRO_T, D), lambda i: (0, i, 0)), row],
        out_shape=[jax.ShapeDtypeStruct((6, M, D), BF16), jax.ShapeDtypeStruct((M, D), F32)],
        scratch_shapes=[pltpu.VMEM((RW_PRO_T + 8, D), F32)],
        compiler_params=_cparams(("arbitrary",)), name="rwkv_prologue",
    )(h, g.reshape(1, D), mu, ovr)


def _rwkv_body(*refs, pb, has_s0):
    r_ref, k_ref, v_ref, a_ref, lw_ref, g_ref, kk_ref, ka_ref, rk_ref, lnw_ref, lnb_ref = refs[:11]
    if has_s0:
        s0_ref, y_ref, sf_ref, st_ref = refs[11:]
    else:
        y_ref, sf_ref, st_ref = refs[11:]
    C = RW_CHUNK
    H = RW_HEAD
    c = pl.program_id(2)

    @pl.when(c == 0)
    def _():
        st_ref[...] = jnp.zeros((pb, 2 * H, 2 * H), F32)
        if has_s0:
            for p in range(pb):
                st_ref[p, 0:H, 0:H] = s0_ref[0, 2 * p]
                st_ref[p, H:2 * H, H:2 * H] = s0_ref[0, 2 * p + 1]

    lane = lax.broadcasted_iota(I32, (C, 2 * H), 1)
    m0 = lane < H
    r2 = lax.broadcasted_iota(I32, (2 * C, 2 * C), 0)
    c2 = lax.broadcasted_iota(I32, (2 * C, 2 * C), 1)
    same = (r2 >= C) == (c2 >= C)
    strict = same & (r2 > c2)
    incl = same & (r2 >= c2)
    eye = (r2 == c2).astype(F32)
    tril = (lax.broadcasted_iota(I32, (C, C), 0) >= lax.broadcasted_iota(I32, (C, C), 1)).astype(F32)

    def per_head(x):
        s0 = jnp.sum(jnp.where(m0, x, 0.0), axis=-1, keepdims=True)
        s1 = jnp.sum(jnp.where(m0, 0.0, x), axis=-1, keepdims=True)
        return jnp.where(m0, s0, s1)

    def stack(z):
        return jnp.concatenate([jnp.where(m0, z, 0.0), jnp.where(m0, 0.0, z)], axis=0).astype(BF16)

    def nt(x, y):
        return lax.dot_general(x, y, NT_DIMS, preferred_element_type=F32)

    def tn(x, y):
        return lax.dot_general(x, y, TN_DIMS, preferred_element_type=F32)

    def mm(x, y):
        return jnp.dot(x, y, preferred_element_type=F32)

    for p in range(pb):
        sl = slice(p * 2 * H, (p + 1) * 2 * H)
        r, k, v, a, lw, g = (ref[:, sl] for ref in (r_ref, k_ref, v_ref, a_ref, lw_ref, g_ref))
        kk = k * kk_ref[:, sl]
        kk = kk / jnp.maximum(jnp.sqrt(per_head(kk * kk)), 1e-12)
        kp = k * (1.0 + (a - 1.0) * ka_ref[:, sl])
        cl = jnp.dot(tril, lw, precision=HIGHEST, preferred_element_type=F32)
        pinv = jnp.exp(-cl)
        pc = jnp.exp(cl[C - 1:C, :])
        at = -kk * jnp.exp(cl - lw)
        bt = kk * a * pinv
        kt = kp * pinv
        a_s, b_s, k_s, r_s, v_s = stack(at), stack(bt), stack(kt), stack(r * jnp.exp(cl)), stack(v)
        x_ab = jnp.where(strict, nt(a_s, b_s), 0.0)
        x_ak = jnp.where(strict, nt(a_s, k_s), 0.0).astype(BF16)
        x_rb = jnp.where(incl, nt(r_s, b_s), 0.0).astype(BF16)
        x_rk = jnp.where(incl, nt(r_s, k_s), 0.0).astype(BF16)
        t_inv = eye + x_ab
        xp = x_ab
        for _ in range(int(math.log2(C)) - 1):
            xp = mm(xp, xp)
            t_inv = t_inv + mm(t_inv, xp)
        st = st_ref[p]
        st_b = st.astype(BF16)
        u_s = mm(t_inv, nt(a_s, st_b) + mm(x_ak, v_s)).astype(BF16)
        y_s = nt(r_s, st_b) + mm(x_rb, u_s) + mm(x_rk, v_s)
        st_ref[p] = st * pc + tn(u_s, stack(bt * pc)) + tn(v_s, stack(kt * pc))
        y = y_s[0:C] + y_s[C:2 * C]
        mean = per_head(y) * (1.0 / H)
        yc = y - mean
        var = per_head(yc * yc) * (1.0 / H)
        yn = yc * lax.rsqrt(var + RW_GN_EPS) * lnw_ref[:, sl] + lnb_ref[:, sl]
        bonus = per_head(r * kp * rk_ref[:, sl]) * v
        y_ref[:, sl] = ((yn + bonus) * g).astype(y_ref.dtype)

    @pl.when(c == pl.num_programs(2) - 1)
    def _():
        for p in range(pb):
            sf_ref[0, 2 * p] = st_ref[p, 0:H, 0:H]
            sf_ref[0, 2 * p + 1] = st_ref[p, H:2 * H, H:2 * H]


def _rwkv_chunks(r, k, v, a, lw, g, params, s0, *, bsz, length, pb=4):
    C = RW_CHUNK
    nch = length // C
    wb = pb * 2 * RW_HEAD
    col = pl.BlockSpec((C, wb), lambda b, pg, c: (b * nch + c, pg))
    par = pl.BlockSpec((1, wb), lambda b, pg, c: (0, pg))
    st_spec = pl.BlockSpec((1, 2 * pb, RW_HEAD, RW_HEAD), lambda b, pg, c: (b, pg, 0, 0))
    in_specs = [col] * 6 + [par] * 5
    args = [r, k, v, a, lw, g] + [x.reshape(1, D) for x in params]
    if s0 is not None:
        in_specs.append(st_spec)
        args.append(s0)
    return pl.pallas_call(
        functools.partial(_rwkv_body, pb=pb, has_s0=s0 is not None),
        grid=(bsz, RW_PAIRS // pb, nch), in_specs=in_specs,
        out_specs=[col, st_spec],
        out_shape=[jax.ShapeDtypeStruct((bsz * length, D), BF16),
                   jax.ShapeDtypeStruct((bsz, 2 * RW_PAIRS, RW_HEAD, RW_HEAD), F32)],
        scratch_shapes=[pltpu.VMEM((pb, 2 * RW_HEAD, 2 * RW_HEAD), F32)],
        compiler_params=_cparams(("parallel", "parallel", "arbitrary")), name="rwkv7_chunks",
    )(*args)


def _mem_block(h, mem_rows, cache_k, cache_v, g_q, g_src, w_q, w_k, w_v, w_o, gq_head, gk_head):
    mn = _rmsnorm(mem_rows, g_src, 256)
    n_mem = mem_rows.shape[0]
    mk = _matmul(mn, w_k, m=n_mem, tm=256, tn=MEM_W, rows=[jnp.tile(gk_head, MEM_HEADS)],
                 epilogue=_ep_headnorm, name="mem_k")
    mv = _matmul(mn, w_v, m=n_mem, tm=256, tn=MEM_W, name="mem_v")
    xq = _rmsnorm(h, g_q, TM_NORM)
    qm = _matmul(xq, w_q, tn=MEM_W, rows=[jnp.tile(gq_head, MEM_HEADS)], epilogue=_ep_headnorm, name="mem_q")
    k_all = jnp.concatenate([mk, cache_k.reshape(B_S * MEM_T, MEM_W)], axis=0)
    v_all = jnp.concatenate([mv, cache_v.reshape(B_S * MEM_T, MEM_W)], axis=0)
    tq = 512
    per_b = L_P // tq
    o_p = _mem_attend(qm, k_all, v_all, n_tiles=M_P // tq, tq=tq, q_rb0=0, kv_of_tile=lambda i: i // per_b)
    o_s = _mem_attend(qm, k_all, v_all, n_tiles=B_S, tq=L_S, q_rb0=M_P // L_S, kv_of_tile=lambda i: i + B_P)
    om = jnp.concatenate([o_p, o_s], axis=0)
    h = _matmul(om, w_o, tiles=[h], epilogue=_ep_residual, name="mem_out")
    return h, mk, mv


def kernel(x_prompt, x_sample, mem_prompt, cache_moba_k, cache_moba_v, page_table, state_hgrn, state_rwkv, state_rwkv_shift, cache_mem_k, cache_mem_v, norm_mix, norm_mem_q, norm_mem_src, norm_ffn, w_even_in, w_even_out, hgrn_lb_logits, hgrn_out_norm, moba_q_norm, moba_k_norm, rwkv_mu, rwkv_w_rkv, rwkv_w_out, rwkv_w0, rwkv_w_lora_a, rwkv_w_lora_b, rwkv_a0, rwkv_a_lora_a, rwkv_a_lora_b, rwkv_g_lora_a, rwkv_g_lora_b, rwkv_k_k, rwkv_k_a, rwkv_r_k, rwkv_ln_w, rwkv_ln_b, mem_w_q, mem_w_k, mem_w_v, mem_w_o, mem_q_norm, mem_k_norm, moe_w_group, moe_w_expert, moe_w_gate, moe_w_up, moe_w_down):
    h = jnp.concatenate([x_prompt.reshape(M_P, D), x_sample.reshape(M_S, D)], axis=0)
    mem_rows = mem_prompt.reshape(B_P * MEM_T, D)
    slopes = jnp.asarray([2.0 ** (-8.0 * (i + 1) / N_MIX_HEADS) for i in range(N_MIX_HEADS)], F32)
    slopes = jnp.broadcast_to(slopes[:, None, None], (N_MIX_HEADS, 1, HD))
    lbs = jnp.cumsum(jax.nn.softmax(hgrn_lb_logits.astype(F32), axis=0), axis=0)
    pt_flat = page_table.reshape(-1).astype(I32)

    xn = _rmsnorm(h, norm_mix[0], TM_NORM)
    w_in = w_even_in[0]
    u = _matmul(xn, w_in, col0=0, n=4 * MIX, name="even_in_hgrn")
    q = _matmul(xn, w_in, col0=4 * MIX, n=MIX, rows=[jnp.tile(moba_q_norm[0], N_MIX_HEADS)],
                epilogue=_ep_headnorm, name="even_in_q")
    k = _matmul(xn, w_in, col0=5 * MIX, n=MIX, rows=[jnp.tile(moba_k_norm[0], N_MIX_HEADS)],
                epilogue=_ep_headnorm, name="even_in_k")
    v = _matmul(xn, w_in, col0=6 * MIX, n=MIX, name="even_in_v")

    oa_p, hg_p = _hgrn(u, lbs[0], hgrn_out_norm[0], None, bsz=B_P, length=L_P, row0=0, C=16, Lc=256)
    oa_s, hg_s = _hgrn(u, lbs[0], hgrn_out_norm[0], state_hgrn[0], bsz=B_S, length=L_S, row0=M_P, C=L_S, Lc=L_S)

    ob_p = _moba_prompt(q, k, v, slopes)
    n_pool = cache_moba_k.shape[1]
    ck = cache_moba_k[0].reshape(n_pool, PAGE, MIX)
    cv = cache_moba_v[0].reshape(n_pool, PAGE, MIX)
    kmean = _moba_kmean(pt_flat, ck)
    sel = _moba_select(q, kmean)
    sel_flat = sel[:, :, :, 0:MOBA_TOPK].reshape(-1)
    ob_s = _moba_sample(pt_flat, sel_flat, q, k, v, ck, cv, slopes)

    mix = jnp.concatenate([jnp.concatenate([oa_p, ob_p], axis=1), jnp.concatenate([oa_s, ob_s], axis=1)], axis=0)
    h = _matmul(mix, w_even_out[0], tiles=[h], epilogue=_ep_residual, name="even_out")
    h, mk0, mv0 = _mem_block(h, mem_rows, cache_mem_k[0], cache_mem_v[0], norm_mem_q[0], norm_mem_src[0],
                             mem_w_q[0], mem_w_k[0], mem_w_v[0], mem_w_o[0], mem_q_norm[0], mem_k_norm[0])
    h = _moe_block(h, norm_ffn[0], moe_w_group[0], moe_w_expert[0], moe_w_gate[0], moe_w_up[0], moe_w_down[0])

    xm, xn1 = _rwkv_prologue(h, norm_mix[1], rwkv_mu[0], state_rwkv_shift[0])
    xm = xm.reshape(6 * M, D)
    r = _matmul(xm, rwkv_w_rkv[0, 0], a_row0=0 * M, name="rwkv_r")
    kr = _matmul(xm, rwkv_w_rkv[0, 1], a_row0=2 * M, name="rwkv_k")
    vr = _matmul(xm, rwkv_w_rkv[0, 2], a_row0=3 * M, name="rwkv_v")
    tw = _matmul(xm, rwkv_w_lora_a[0], a_row0=1 * M, tn=128, epilogue=_ep_tanh, out_dtype=BF16, name="rwkv_w_a")
    lw = _matmul(tw, rwkv_w_lora_b[0], rows=[rwkv_w0[0]], epilogue=_ep_log_decay, name="rwkv_w_b")
    ta = _matmul(xm, rwkv_a_lora_a[0], a_row0=4 * M, tn=128, out_dtype=BF16, name="rwkv_a_a")
    av = _matmul(ta, rwkv_a_lora_b[0], rows=[rwkv_a0[0]], epilogue=_ep_bias_sigmoid, name="rwkv_a_b")
    gl = rwkv_g_lora_a.shape[2]
    glp = -(-gl // 128) * 128
    g_la = jnp.pad(rwkv_g_lora_a[0], ((0, 0), (0, glp - gl)))
    g_lb = jnp.pad(rwkv_g_lora_b[0], ((0, glp - gl), (0, 0)))
    tg = _matmul(xm, g_la, a_row0=5 * M, tn=glp, epilogue=_ep_sigmoid, out_dtype=BF16, name="rwkv_g_a")
    gv = _matmul(tg, g_lb, name="rwkv_g_b")

    params = (rwkv_k_k[0], rwkv_k_a[0], rwkv_r_k[0].reshape(D), rwkv_ln_w[0], rwkv_ln_b[0])
    seq = (r, kr, vr, av, lw, gv)
    yo_p, rw_p = _rwkv_chunks(*seq, params, None, bsz=B_P, length=L_P)

    def pad_sample(x):
        xs = x[M_P:].reshape(B_S, L_S, D)
        return jnp.pad(xs, ((0, 0), (0, RW_CHUNK - L_S), (0, 0))).reshape(B_S * RW_CHUNK, D)

    yo_s, rw_s = _rwkv_chunks(*[pad_sample(x) for x in seq], params, state_rwkv[0], bsz=B_S, length=RW_CHUNK)
    yo_s = yo_s.reshape(B_S, RW_CHUNK, D)[:, :L_S].reshape(M_S, D)
    yo = jnp.concatenate([yo_p, yo_s], axis=0)
    h = _matmul(yo, rwkv_w_out[0], tiles=[h], epilogue=_ep_residual, name="rwkv_out")
    h, mk1, mv1 = _mem_block(h, mem_rows, cache_mem_k[1], cache_mem_v[1], norm_mem_q[1], norm_mem_src[1],
                             mem_w_q[1], mem_w_k[1], mem_w_v[1], mem_w_o[1], mem_q_norm[1], mem_k_norm[1])
    h = _moe_block(h, norm_ffn[1], moe_w_group[1], moe_w_expert[1], moe_w_gate[1], moe_w_up[1], moe_w_down[1])

    y_prompt = h[:M_P].reshape(B_P, L_P, D)
    y_sample = h[M_P:].reshape(B_S, L_S, D)
    kv_p = (B_P, L_P, N_MIX_HEADS, HD)
    kv_s = (B_S, L_S, N_MIX_HEADS, HD)
    mem_shape = (B_P, MEM_T, MEM_HEADS, HD)
    sh_p = xn1[L_P - 1:M_P:L_P]
    sh_s = xn1[M_P + L_S - 1::L_S]
    return (y_prompt, y_sample,
            k[:M_P].reshape(kv_p)[None], v[:M_P].reshape(kv_p)[None], hg_p[None], rw_p[None], sh_p[None],
            jnp.stack([mk0.reshape(mem_shape), mk1.reshape(mem_shape)]),
            jnp.stack([mv0.reshape(mem_shape), mv1.reshape(mem_shape)]),
            k[M_P:].reshape(kv_s)[None], v[M_P:].reshape(kv_s)[None], hg_s[None], rw_s[None], sh_s[None])
```
